```python
import math
import jax, jax.numpy as jnp
from jax import lax
import numpy as np

D_MODEL = 2048
BATCH = 1
SEQ = 16384
DEPTH = 1
DEC_BATCH = 32
DEC_SEQ = 32
PAST_LEN = 1024

CHUNK = 64
EPS = 1e-6
ATTN_HEADS = 8
KV_HEADS = 2
HEAD_DIM = 128
ATTN_WIDTH = ATTN_HEADS * HEAD_DIM
KV_WIDTH = KV_HEADS * HEAD_DIM
IDX_HEADS = 16
IDX_DIM = 64
TOPK_MAX = 256
Q_BLOCK = 128
ROPE_THETA = 10000.0
SSD_WIDTH = D_MODEL - ATTN_WIDTH
SSD_HEADDIM = 64
SSD_HEADS = SSD_WIDTH // SSD_HEADDIM
SSD_GROUPS = 2
SSD_STATE = 128
CONV_W = 4
CONV_DIM = SSD_WIDTH + 2 * SSD_GROUPS * SSD_STATE
SSD_CHUNK = CHUNK
IN_COLS = ATTN_WIDTH + 2 * KV_WIDTH + IDX_HEADS * IDX_DIM + IDX_DIM + IDX_HEADS + SSD_WIDTH + CONV_DIM + SSD_HEADS
N_EXPERTS = 32
TOP_K = 4
D_FF = D_MODEL
SWIGLU_LIMIT = 7.0
SWIGLU_ALPHA = 1.702
MOE_BLOCK = 128

kernel_name = 'hybrid_dsa_ssd_moe_streaming_step'


def _split_in(u):
    sizes = (ATTN_WIDTH, KV_WIDTH, KV_WIDTH, IDX_HEADS * IDX_DIM, IDX_DIM, IDX_HEADS, SSD_WIDTH, CONV_DIM, SSD_HEADS)
    return jnp.split(u, np.cumsum(sizes)[:-1].tolist(), axis=-1)


def rms_norm(x, g):
    xf = x.astype(jnp.float32)
    y = xf * lax.rsqrt(jnp.mean(xf * xf, axis=-1, keepdims=True) + EPS)
    return (y * g.astype(jnp.float32)).astype(x.dtype)


def rope(x, pos):
    half = x.shape[-1] // 2
    inv_freq = ROPE_THETA ** (-jnp.arange(half, dtype=jnp.float32) / half)
    ang = pos.astype(jnp.float32)[:, None] * inv_freq[None, :]
    cos = jnp.cos(ang)[:, None, :]
    sin = jnp.sin(ang)[:, None, :]
    xf = x.astype(jnp.float32)
    x1, x2 = xf[..., :half], xf[..., half:]
    return jnp.concatenate([x1 * cos - x2 * sin, x1 * sin + x2 * cos], axis=-1).astype(x.dtype)


def sparse_attention(q, qi, wi, q_pos, k_all, v_all, ki_all, k_pos, topk):
    f32 = jnp.float32
    idx_logits = jnp.einsum('bthd,bsd->bths', qi.astype(f32), ki_all.astype(f32))
    w = wi.astype(f32) * (IDX_HEADS ** -0.5) * (IDX_DIM ** -0.5)
    score = jnp.einsum('bths,bth->bts', jax.nn.relu(idx_logits), w)
    allowed = (k_pos[None, :] // CHUNK) <= (q_pos[:, None] // CHUNK)
    score = jnp.where(allowed[None], score, -jnp.inf)
    _, sel = lax.top_k(score, topk)
    valid = (jnp.take(k_pos, sel) // CHUNK) <= (q_pos[None, :, None] // CHUNK)
    gather = jax.vmap(lambda arr, ix: arr[ix])
    k_sel = gather(k_all, sel)
    v_sel = gather(v_all, sel)
    b, t = q.shape[:2]
    qg = q.reshape(b, t, KV_HEADS, ATTN_HEADS // KV_HEADS, HEAD_DIM).astype(f32)
    logits = jnp.einsum('btgrd,btkgd->btgrk', qg, k_sel.astype(f32)) * (HEAD_DIM ** -0.5)
    logits = jnp.where(valid[:, :, None, None, :], logits, -jnp.inf)
    p = jax.nn.softmax(logits, axis=-1)
    o = jnp.einsum('btgrk,btkgd->btgrd', p, v_sel.astype(f32))
    return o.reshape(b, t, ATTN_WIDTH)


def causal_conv(xbc, conv_state, w, bias):
    t = xbc.shape[1]
    xp = jnp.concatenate([conv_state.astype(xbc.dtype), xbc], axis=1)
    y = bias.astype(jnp.float32) + xp[:, 0:t].astype(jnp.float32) * w[0]
    for i in range(1, CONV_W):
        y = y + xp[:, i:i + t].astype(jnp.float32) * w[i]
    return jax.nn.silu(y), xp[:, -(CONV_W - 1):]


def ssd_scan(x, dt, a_log, bm, cm, init_state, chunk):
    f32 = jnp.float32
    b, T = x.shape[:2]
    nc = T // chunk
    G, R, P, N = SSD_GROUPS, SSD_HEADS // SSD_GROUPS, SSD_HEADDIM, SSD_STATE
    A = -jnp.exp(a_log.astype(f32))
    a = (dt * A).reshape(b, nc, chunk, G, R)
    xdt = (x.astype(f32) * dt[..., None]).reshape(b, nc, chunk, G, R, P)
    Bc = bm.astype(f32).reshape(b, nc, chunk, G, N)
    Cc = cm.astype(f32).reshape(b, nc, chunk, G, N)
    acum = jnp.cumsum(a, axis=2)
    seg = acum[:, :, :, None] - acum[:, :, None, :]
    causal = jnp.tril(jnp.ones((chunk, chunk), bool))[None, None, :, :, None, None]
    lmat = jnp.exp(jnp.where(causal, seg, -jnp.inf))
    cb = jnp.einsum('bcign,bcjgn->bcijg', Cc, Bc)
    y_diag = jnp.einsum('bcijgr,bcjgrp->bcigrp', cb[..., None] * lmat, xdt)
    decay = jnp.exp(acum[:, :, -1:] - acum)
    states = jnp.einsum('bcjgn,bcjgrp->bcgrpn', Bc, decay[..., None] * xdt)
    chunk_decay = jnp.exp(acum[:, :, -1])
    s0 = init_state.astype(f32).reshape(b, G, R, P, N)

    def step(s, inp):
        dec, st = inp
        return dec[..., None, None] * s + st, s

    s_final, s_prev = lax.scan(step, s0, (jnp.moveaxis(chunk_decay, 1, 0), jnp.moveaxis(states, 1, 0)))
    s_prev = jnp.moveaxis(s_prev, 0, 1)
    y_off = jnp.einsum('bcign,bcgrpn->bcigrp', Cc, s_prev) * jnp.exp(acum)[..., None]
    y = (y_diag + y_off).reshape(b, T, SSD_HEADS, P)
    return y, s_final.reshape(b, SSD_HEADS, P, N)


def ssd_mixer(z, xbc, dt_raw, conv_state, ssm_state, chunk, conv_w, conv_b, dt_bias, a_log, d_skip, norm_g):
    f32 = jnp.float32
    b, t = z.shape[:2]
    xbc_c, new_conv = causal_conv(xbc, conv_state, conv_w, conv_b)
    xs, bm, cm = jnp.split(xbc_c, [SSD_WIDTH, SSD_WIDTH + SSD_GROUPS * SSD_STATE], axis=-1)
    xs = xs.reshape(b, t, SSD_HEADS, SSD_HEADDIM)
    bm = bm.reshape(b, t, SSD_GROUPS, SSD_STATE)
    cm = cm.reshape(b, t, SSD_GROUPS, SSD_STATE)
    dt = jax.nn.softplus(dt_raw.astype(f32) + dt_bias.astype(f32))
    y, new_ssm = ssd_scan(xs, dt, a_log, bm, cm, ssm_state, chunk)
    y = y + d_skip.astype(f32)[:, None] * xs
    y = y.reshape(b, t, SSD_WIDTH) * jax.nn.silu(z.astype(f32))
    yg = y.reshape(b, t, SSD_GROUPS, SSD_WIDTH // SSD_GROUPS)
    yg = yg * lax.rsqrt(jnp.mean(yg * yg, axis=-1, keepdims=True) + EPS)
    y = yg.reshape(b, t, SSD_WIDTH) * norm_g.astype(f32)
    return y, new_conv, new_ssm.astype(ssm_state.dtype)


def clamped_swiglu(g, u):
    g = jnp.minimum(g, SWIGLU_LIMIT)
    u = jnp.clip(u, -SWIGLU_LIMIT, SWIGLU_LIMIT)
    return g * jax.nn.sigmoid(SWIGLU_ALPHA * g) * (u + 1.0)


def moe(h, w_router, b_router, w_mlp1, b_mlp1, w_mlp2, b_mlp2):
    b, t, d = h.shape
    hf = h.reshape(-1, d)
    n = hf.shape[0]
    logits = (hf @ w_router + b_router).astype(jnp.float32)
    top_val, top_idx = lax.top_k(logits, TOP_K)
    gate = jax.nn.softmax(top_val, axis=-1)
    flat_e = top_idx.reshape(-1)
    flat_w = gate.reshape(-1)
    flat_tok = jnp.arange(n * TOP_K, dtype=jnp.int32) // TOP_K
    order = jnp.argsort(flat_e)
    se = flat_e[order]
    counts = jnp.bincount(flat_e, length=N_EXPERTS)
    padded = (counts + MOE_BLOCK - 1) // MOE_BLOCK * MOE_BLOCK
    start = jnp.cumsum(counts) - counts
    pend = jnp.cumsum(padded)
    pstart = pend - padded
    dest = pstart[se] + jnp.arange(n * TOP_K, dtype=jnp.int32) - start[se]
    n_rows = (n * TOP_K + MOE_BLOCK - 1) // MOE_BLOCK * MOE_BLOCK + N_EXPERTS * MOE_BLOCK
    n_blk = n_rows // MOE_BLOCK
    row_tok = jnp.zeros((n_rows,), jnp.int32).at[dest].set(flat_tok[order])
    row_w = jnp.zeros((n_rows,), jnp.float32).at[dest].set(flat_w[order])
    blk_e = jnp.minimum(jnp.searchsorted(pend, jnp.arange(n_blk, dtype=pend.dtype) * MOE_BLOCK, side='right'), N_EXPERTS - 1)

    def expert_block(args):
        e, tok, w = args
        xb = hf[tok]
        u = xb @ w_mlp1[e] + b_mlp1[e]
        act = clamped_swiglu(u[:, :D_FF], u[:, D_FF:])
        y = act @ w_mlp2[e] + b_mlp2[e]
        return (y.astype(jnp.float32) * w[:, None]).astype(hf.dtype)

    yb = lax.map(expert_block, (blk_e, row_tok.reshape(n_blk, MOE_BLOCK), row_w.reshape(n_blk, MOE_BLOCK)))
    out = jnp.zeros_like(hf).at[row_tok].add(yb.reshape(n_rows, d))
    return out.reshape(b, t, d)


def setup_inputs(seed: int = 0) -> dict:
    key = jax.random.key(seed)
    ks = jax.random.split(key, 32)
    f32 = jnp.float32
    L = DEPTH

    def nrm(k, shape, s=1.0):
        return jax.random.normal(k, shape, f32) * s

    dt0 = jnp.exp(jax.random.uniform(ks[17], (L, SSD_HEADS), f32, math.log(1e-3), math.log(1e-1)))
    return {
        'x_prompt': nrm(ks[0], (BATCH, SEQ, D_MODEL)),
        'x_sample': nrm(ks[1], (DEC_BATCH, DEC_SEQ, D_MODEL)),
        'c_prompt': nrm(ks[2], (BATCH, D_MODEL)),
        'c_sample': nrm(ks[3], (DEC_BATCH, D_MODEL)),
        'cache_k': nrm(ks[4], (L, DEC_BATCH, PAST_LEN, KV_HEADS, HEAD_DIM)),
        'cache_v': nrm(ks[5], (L, DEC_BATCH, PAST_LEN, KV_HEADS, HEAD_DIM)),
        'cache_idx_k': nrm(ks[6], (L, DEC_BATCH, PAST_LEN, IDX_DIM)),
        'state_conv': nrm(ks[7], (L, DEC_BATCH, CONV_W - 1, CONV_DIM)),
        'state_ssm': nrm(ks[8], (L, DEC_BATCH, SSD_HEADS, SSD_HEADDIM, SSD_STATE), 0.5),
        'w_ada': nrm(ks[9], (L, D_MODEL, 6 * D_MODEL), D_MODEL ** -0.5),
        'b_ada': nrm(ks[10], (L, 6 * D_MODEL), 0.01),
        'mix_norm_g': 1.0 + nrm(ks[11], (L, D_MODEL), 0.01),
        'ffn_norm_g': 1.0 + nrm(ks[12], (L, D_MODEL), 0.01),
        'w_in': nrm(ks[13], (L, D_MODEL, IN_COLS), D_MODEL ** -0.5),
        'q_norm_g': 1.0 + nrm(ks[14], (L, HEAD_DIM), 0.01),
        'k_norm_g': 1.0 + nrm(ks[15], (L, HEAD_DIM), 0.01),
        'conv_w': nrm(ks[16], (L, CONV_W, CONV_DIM), CONV_W ** -0.5),
        'conv_b': nrm(ks[18], (L, CONV_DIM), 0.01),
        'dt_bias': dt0 + jnp.log(-jnp.expm1(-dt0)),
        'a_log': jnp.log(jax.random.uniform(ks[19], (L, SSD_HEADS), f32, 1.0, 16.0)),
        'd_skip': 1.0 + nrm(ks[20], (L, SSD_HEADS), 0.01),
        'ssd_norm_g': 1.0 + nrm(ks[21], (L, SSD_WIDTH), 0.01),
        'w_out': nrm(ks[22], (L, D_MODEL, D_MODEL), D_MODEL ** -0.5),
        'w_router': nrm(ks[23], (L, D_MODEL, N_EXPERTS), D_MODEL ** -0.5),
        'b_router': nrm(ks[24], (L, N_EXPERTS), 0.01),
        'w_mlp1': nrm(ks[25], (L, N_EXPERTS, D_MODEL, 2 * D_FF), D_MODEL ** -0.5),
        'b_mlp1': nrm(ks[26], (L, N_EXPERTS, 2 * D_FF), 0.01),
        'w_mlp2': nrm(ks[27], (L, N_EXPERTS, D_FF, D_MODEL), D_FF ** -0.5),
        'b_mlp2': nrm(ks[28], (L, N_EXPERTS, D_MODEL), 0.01),
    }


def reference(x_prompt, x_sample, c_prompt, c_sample, cache_k, cache_v, cache_idx_k, state_conv, state_ssm,
              w_ada, b_ada, mix_norm_g, ffn_norm_g, w_in, q_norm_g, k_norm_g, conv_w, conv_b, dt_bias, a_log,
              d_skip, ssd_norm_g, w_out, w_router, b_router, w_mlp1, b_mlp1, w_mlp2, b_mlp2):
    b_p, t_p, _ = x_prompt.shape
    b_s, t_s, _ = x_sample.shape
    past = cache_k.shape[2]
    pos_p = jnp.arange(t_p, dtype=jnp.int32)
    pos_s = past + jnp.arange(t_s, dtype=jnp.int32)
    kpos_s = jnp.arange(past + t_s, dtype=jnp.int32)
    topk_p = min(TOPK_MAX, t_p // 4)
    topk_s = min(TOPK_MAX, (past + t_s) // 4)
    n_qblk = t_p // Q_BLOCK

    def attend_prompt(q, k, v, qi, ki, wi):
        def blocks(a):
            return jnp.moveaxis(a.reshape(a.shape[0], n_qblk, Q_BLOCK, *a.shape[2:]), 1, 0)

        def one(args):
            qb, qib, wib, pb = args
            return sparse_attention(qb, qib, wib, pb, k, v, ki, pos_p, topk_p)

        o = lax.map(one, (blocks(q), blocks(qi), blocks(wi), pos_p.reshape(n_qblk, Q_BLOCK)))
        return jnp.moveaxis(o, 0, 1).reshape(b_p, t_p, ATTN_WIDTH)

    def layer(l, x, c, pos, attend, conv_state, ssm_state, ssd_chunk):
        mod = (jax.nn.silu(c) @ w_ada[l] + b_ada[l])[:, None, :]
        sh_a, sc_a, g_a, sh_m, sc_m, g_m = jnp.split(mod, 6, axis=-1)
        h = rms_norm(x, mix_norm_g[l]) * (1.0 + sc_a) + sh_a
        q, k, v, qi, ki, wi, z, xbc, dt = _split_in(h @ w_in[l])
        bsz, t = x.shape[:2]
        q = rope(rms_norm(q.reshape(bsz, t, ATTN_HEADS, HEAD_DIM), q_norm_g[l]), pos)
        k = rope(rms_norm(k.reshape(bsz, t, KV_HEADS, HEAD_DIM), k_norm_g[l]), pos)
        v = v.reshape(bsz, t, KV_HEADS, HEAD_DIM)
        qi = rope(qi.reshape(bsz, t, IDX_HEADS, IDX_DIM), pos)
        ki = rope(ki[:, :, None, :], pos)[:, :, 0, :]
        o_attn = attend(q, k, v, qi, ki, wi)
        o_ssd, new_conv, new_ssm = ssd_mixer(z, xbc, dt, conv_state, ssm_state, ssd_chunk, conv_w[l], conv_b[l],
                                             dt_bias[l], a_log[l], d_skip[l], ssd_norm_g[l])
        mixed = jnp.concatenate([o_attn.astype(x.dtype), o_ssd.astype(x.dtype)], axis=-1)
        x = x + g_a * (mixed @ w_out[l])
        h = rms_norm(x, ffn_norm_g[l]) * (1.0 + sc_m) + sh_m
        x = x + g_m * moe(h, w_router[l], b_router[l], w_mlp1[l], b_mlp1[l], w_mlp2[l], b_mlp2[l])
        return x, (k, v, ki, new_conv, new_ssm)

    y_p = x_prompt
    y_s = x_sample
    kp, vp, kip, cvp, smp = [], [], [], [], []
    ks_, vs_, kis, cvs, sms = [], [], [], [], []
    for l in range(DEPTH):
        zero_conv = jnp.zeros((b_p, CONV_W - 1, CONV_DIM), x_prompt.dtype)
        zero_ssm = jnp.zeros((b_p, SSD_HEADS, SSD_HEADDIM, SSD_STATE), state_ssm.dtype)
        y_p, st_p = layer(l, y_p, c_prompt, pos_p, attend_prompt, zero_conv, zero_ssm, SSD_CHUNK)

        def attend_sample(q, k, v, qi, ki, wi, l=l):
            k_all = jnp.concatenate([cache_k[l].astype(k.dtype), k], axis=1)
            v_all = jnp.concatenate([cache_v[l].astype(v.dtype), v], axis=1)
            ki_all = jnp.concatenate([cache_idx_k[l].astype(ki.dtype), ki], axis=1)
            return sparse_attention(q, qi, wi, pos_s, k_all, v_all, ki_all, kpos_s, topk_s)

        y_s, st_s = layer(l, y_s, c_sample, pos_s, attend_sample, state_conv[l], state_ssm[l], t_s)
        kp.append(st_p[0]); vp.append(st_p[1]); kip.append(st_p[2]); cvp.append(st_p[3]); smp.append(st_p[4])
        ks_.append(st_s[0]); vs_.append(st_s[1]); kis.append(st_s[2]); cvs.append(st_s[3]); sms.append(st_s[4])

    return (y_p, y_s,
            jnp.stack(kp), jnp.stack(vp), jnp.stack(kip), jnp.stack(cvp), jnp.stack(smp),
            jnp.stack(ks_), jnp.stack(vs_), jnp.stack(kis), jnp.stack(cvs), jnp.stack(sms))
```

```python
import functools
import math

import jax
import jax.numpy as jnp
import numpy as np
from jax import lax
from jax.experimental import pallas as pl
from jax.experimental.pallas import tpu as pltpu

f32 = jnp.float32
bf16 = jnp.bfloat16
i32 = jnp.int32

D_MODEL = 2048
CHUNK = 64
EPS = 1e-6
ATTN_HEADS = 8
KV_HEADS = 2
HEAD_DIM = 128
ATTN_WIDTH = ATTN_HEADS * HEAD_DIM
KV_WIDTH = KV_HEADS * HEAD_DIM
IDX_HEADS = 16
IDX_DIM = 64
TOPK_MAX = 256
ROPE_THETA = 10000.0
SSD_WIDTH = D_MODEL - ATTN_WIDTH
SSD_HEADDIM = 64
SSD_HEADS = SSD_WIDTH // SSD_HEADDIM
SSD_GROUPS = 2
SSD_STATE = 128
CONV_W = 4
CONV_DIM = SSD_WIDTH + 2 * SSD_GROUPS * SSD_STATE
N_EXPERTS = 32
TOP_K = 4
D_FF = D_MODEL
SWIGLU_LIMIT = 7.0
SWIGLU_ALPHA = 1.702

LANES_V7X = 128
SUBLANES_V7X = 8
VMEM_BYTES_V7X = 64 * 1024 * 1024
MIB = 1024 * 1024

GROUP = 32
U_Q, U_QI, U_Z, U_XBC = 0, 1024, 2048, 3072
U_K, U_V, U_KI, U_WI, U_DT = 4608, 4864, 5120, 5248, 5376
U_COLS = 5632
IN_TM = 1024
IN_TN = 512
POST_TM = 512
OUT_TM = 512
NEG = -1e30

MOE_SUB = 256
MOE_SUPER = 4
MOE_FFC = 512
TOK_TILE = 256


def _params(sem, vmem_bytes):
    return pltpu.CompilerParams(dimension_semantics=sem, vmem_limit_bytes=int(min(vmem_bytes, VMEM_BYTES_V7X - 4 * MIB)))


def _split3(x):
    hi = x.astype(bf16)
    r1 = x - hi.astype(f32)
    mid = r1.astype(bf16)
    lo = (r1 - mid.astype(f32)).astype(bf16)
    return hi, mid, lo


def _dot_nt(a, b):
    return lax.dot_general(a, b, (((1,), (1,)), ((), ())), preferred_element_type=f32)


def _dot(a, b):
    return jnp.dot(a, b, preferred_element_type=f32)


def _ada_kernel(c_ref, w_ref, b_ref, o_ref):
    c = c_ref[...]
    a = c * jax.nn.sigmoid(c)
    a_hi, a_mid, _ = _split3(a)
    w = w_ref[...]
    w_hi, w_mid, _ = _split3(w)
    o_ref[...] = (_dot(a_hi, w_hi) + _dot(a_hi, w_mid) + _dot(a_mid, w_hi)) + b_ref[...]


def _ada(c_all, w_ada, b_ada):
    rows, d = c_all.shape
    n = w_ada.shape[1]
    tn = 1024
    return pl.pallas_call(
        _ada_kernel,
        grid=(n // tn,),
        in_specs=[pl.BlockSpec((rows, d), lambda j: (0, 0)),
                  pl.BlockSpec((d, tn), lambda j: (0, j)),
                  pl.BlockSpec((1, tn), lambda j: (0, j))],
        out_specs=pl.BlockSpec((rows, tn), lambda j: (0, j)),
        out_shape=jax.ShapeDtypeStruct((rows, n), f32),
        compiler_params=_params(("arbitrary",), 40 * MIB),
        name="ada_mod",
    )(c_all, w_ada, b_ada.reshape(1, n))


def _inproj_kernel(x_ref, sc_ref, sh_ref, g_ref, w_ref, o_ref, h_ref):
    @pl.when(pl.program_id(1) == 0)
    def _():
        x = x_ref[...]
        y = x * lax.rsqrt(jnp.mean(x * x, axis=-1, keepdims=True) + EPS) * g_ref[...]
        h = y * (1.0 + sc_ref[...]) + sh_ref[...]
        h_ref[...] = h.reshape(h_ref.shape).astype(bf16)

    o_ref[...] = _dot(h_ref[...], w_ref[...])


def _inproj(x_g, sc_g, sh_g, gain, w_pack):
    ng, _, d = x_g.shape
    gb = IN_TM // GROUP
    ncol = w_pack.shape[1]
    return pl.pallas_call(
        _inproj_kernel,
        grid=(ng // gb, ncol // IN_TN),
        in_specs=[pl.BlockSpec((gb, GROUP, d), lambda i, j: (i, 0, 0)),
                  pl.BlockSpec((gb, 1, d), lambda i, j: (i, 0, 0)),
                  pl.BlockSpec((gb, 1, d), lambda i, j: (i, 0, 0)),
                  pl.BlockSpec((1, 1, d), lambda i, j: (0, 0, 0)),
                  pl.BlockSpec((d, IN_TN), lambda i, j: (0, j))],
        out_specs=pl.BlockSpec((IN_TM, IN_TN), lambda i, j: (i, j)),
        out_shape=jax.ShapeDtypeStruct((ng * GROUP, ncol), f32),
        scratch_shapes=[pltpu.VMEM((IN_TM, d), bf16)],
        compiler_params=_params(("arbitrary", "arbitrary"), 40 * MIB),
        name="in_proj",
    )(x_g, sc_g, sh_g, gain.reshape(1, 1, d), w_pack)


def _post_kernel(q_ref, qi_ref, k_ref, v_ref, ki_ref, wi_ref, cq_ref, sq_ref, ci_ref, si_ref, gq_ref, gk_ref,
                 qh_ref, qih_ref, kf_ref, kb_ref, vf_ref, vb_ref, kif_ref, kib_ref, ws_ref):
    cq, sq, ci, si = cq_ref[...], sq_ref[...], ci_ref[...], si_ref[...]
    tm = cq.shape[0]
    lane = lax.broadcasted_iota(i32, (tm, LANES_V7X), 1)
    first_half = (lane % IDX_DIM) < (IDX_DIM // 2)

    def norm_rope(x, g):
        y = x * lax.rsqrt(jnp.mean(x * x, axis=-1, keepdims=True) + EPS) * g
        return y * cq + pltpu.roll(y, HEAD_DIM // 2, 1) * sq

    def idx_rope(x):
        partner = jnp.where(first_half, pltpu.roll(x, LANES_V7X - IDX_DIM // 2, 1), pltpu.roll(x, IDX_DIM // 2, 1))
        return x * ci + partner * si

    gq, gk = gq_ref[...], gk_ref[...]
    q = q_ref[...]
    for h in range(ATTN_HEADS):
        r = norm_rope(q[:, h * HEAD_DIM:(h + 1) * HEAD_DIM], gq)
        qh_ref[h] = (r * (HEAD_DIM ** -0.5)).astype(bf16)
    k = k_ref[...]
    for h in range(KV_HEADS):
        r = norm_rope(k[:, h * HEAD_DIM:(h + 1) * HEAD_DIM], gk)
        kf_ref[:, h * HEAD_DIM:(h + 1) * HEAD_DIM] = r
        kb_ref[:, h * HEAD_DIM:(h + 1) * HEAD_DIM] = r.astype(bf16)
    v = v_ref[...]
    vf_ref[...] = v
    vb_ref[...] = v.astype(bf16)
    qi = qi_ref[...]
    for c in range(IDX_HEADS // 2):
        r = idx_rope(qi[:, c * LANES_V7X:(c + 1) * LANES_V7X]).astype(bf16)
        qih_ref[2 * c] = r[:, :IDX_DIM]
        qih_ref[2 * c + 1] = r[:, IDX_DIM:]
    r = idx_rope(ki_ref[...])[:, :IDX_DIM]
    kif_ref[...] = r
    kib_ref[...] = r.astype(bf16)
    ws_ref[...] = wi_ref[...] * ((IDX_HEADS ** -0.5) * (IDX_DIM ** -0.5))


def _post(u, cq, sq, ci, si, gq, gk):
    t = u.shape[0]
    tm = POST_TM

    def ublk(width, off):
        return pl.BlockSpec((tm, width), lambda i, o=off // width: (i, o))

    tab = pl.BlockSpec((tm, LANES_V7X), lambda i: (i, 0))
    gain = pl.BlockSpec((1, HEAD_DIM), lambda i: (0, 0))
    row = lambda w: pl.BlockSpec((tm, w), lambda i: (i, 0))
    return pl.pallas_call(
        _post_kernel,
        grid=(t // tm,),
        in_specs=[ublk(ATTN_WIDTH, U_Q), ublk(IDX_HEADS * IDX_DIM, U_QI), ublk(KV_WIDTH, U_K), ublk(KV_WIDTH, U_V),
                  ublk(LANES_V7X, U_KI), ublk(LANES_V7X, U_WI), tab, tab, tab, tab, gain, gain],
        out_specs=[pl.BlockSpec((ATTN_HEADS, tm, HEAD_DIM), lambda i: (0, i, 0)),
                   pl.BlockSpec((IDX_HEADS, tm, IDX_DIM), lambda i: (0, i, 0)),
                   row(KV_WIDTH), row(KV_WIDTH), row(KV_WIDTH), row(KV_WIDTH), row(IDX_DIM), row(IDX_DIM),
                   row(LANES_V7X)],
        out_shape=[jax.ShapeDtypeStruct((ATTN_HEADS, t, HEAD_DIM), bf16),
                   jax.ShapeDtypeStruct((IDX_HEADS, t, IDX_DIM), bf16),
                   jax.ShapeDtypeStruct((t, KV_WIDTH), f32), jax.ShapeDtypeStruct((t, KV_WIDTH), bf16),
                   jax.ShapeDtypeStruct((t, KV_WIDTH), f32), jax.ShapeDtypeStruct((t, KV_WIDTH), bf16),
                   jax.ShapeDtypeStruct((t, IDX_DIM), f32), jax.ShapeDtypeStruct((t, IDX_DIM), bf16),
                   jax.ShapeDtypeStruct((t, LANES_V7X), f32)],
        compiler_params=_params(("arbitrary",), 40 * MIB),
        name="qk_post",
    )(u, u, u, u, u, u, cq, sq, ci, si, gq.reshape(1, HEAD_DIM), gk.reshape(1, HEAD_DIM))


def _attn_kernel(q_ref, qi_ref, w_ref, k_ref, v_ref, ki_ref, o_ref, s_ref, b_ref, *,
                 tq, tk, n_keys, q_base, q_stride):
    if q_stride:
        q0 = q_base + pl.program_id(0) * q_stride
        max_limit = jnp.minimum(n_keys, (lax.shift_right_logical(q0 + tq - 1, 6) + 1) * CHUNK)
        n_kt = lax.div(max_limit + tk - 1, tk)
    else:
        q0 = q_base
        n_kt = (min(n_keys, ((q0 + tq - 1) // CHUNK + 1) * CHUNK) + tk - 1) // tk
    qpos = q0 + lax.broadcasted_iota(i32, (tq, 1), 0)
    limit = jnp.minimum(n_keys, (lax.shift_right_logical(qpos, 6) + 1) * CHUNK)
    kk = jnp.minimum(TOPK_MAX, limit).astype(f32)
    nlane = tk // LANES_V7X

    def tile_off(kt):
        return pl.multiple_of(kt * tk, tk)

    qi = qi_ref[...].reshape(IDX_HEADS * tq, IDX_DIM)
    w = w_ref[...]

    def score_tile(kt, c):
        off = tile_off(kt)
        logit = _dot_nt(qi, ki_ref[0, pl.ds(off, tk), :])
        acc = jnp.zeros((tq, tk), f32)
        for h in range(IDX_HEADS):
            acc = acc + jnp.maximum(logit[h * tq:(h + 1) * tq], 0.0) * w[:, h:h + 1]
        kpos = off + lax.broadcasted_iota(i32, (tq, tk), 1)
        acc = jnp.where(kpos < limit, acc, -jnp.inf)
        bits = pltpu.bitcast(acc, i32)
        s_ref[:, pl.ds(off, tk)] = jnp.where(bits < 0, bits ^ 0x7FFFFFFF, bits)
        return c

    lax.fori_loop(0, n_kt, score_tile, 0)

    def count_ge(cand):
        def body(kt, acc):
            off = tile_off(kt)
            for c in range(nlane):
                blk = s_ref[:, pl.ds(off + c * LANES_V7X, LANES_V7X)]
                acc = acc + jnp.where(blk >= cand, 1.0, 0.0)
            return acc

        acc = lax.fori_loop(0, n_kt, body, jnp.zeros((tq, LANES_V7X), f32))
        return jnp.sum(acc, axis=1, keepdims=True)

    int_min = jnp.int32(-2 ** 31)
    prefix = jnp.where(count_ge(jnp.zeros((tq, 1), i32)) >= kk, jnp.int32(0), int_min)

    def bit_step(i, prefix):
        cand = prefix + lax.shift_left(jnp.int32(1), 30 - i)
        return jnp.where(count_ge(cand) >= kk, cand, prefix)

    thr = lax.fori_loop(0, 31, bit_step, prefix)
    n_gt = count_ge(thr + 1)
    n_ge = count_ge(thr)
    take_eq = kk - n_gt
    tie_excess = jnp.max(jnp.where(n_ge > kk, 1.0, 0.0))

    def bias_plain():
        def body(kt, c):
            off = tile_off(kt)
            b_ref[:, pl.ds(off, tk)] = jnp.where(s_ref[:, pl.ds(off, tk)] >= thr, 0.0, NEG)
            return c

        lax.fori_loop(0, n_kt, body, 0)

    def bias_ties():
        before = (lax.broadcasted_iota(i32, (tk, tk), 0) < lax.broadcasted_iota(i32, (tk, tk), 1))
        before = jnp.where(before, 1.0, 0.0).astype(bf16)

        def body(kt, seen):
            off = tile_off(kt)
            blk = s_ref[:, pl.ds(off, tk)]
            eq = jnp.where(blk == thr, 1.0, 0.0)
            rank = _dot(eq.astype(bf16), before) + seen
            keep_eq = jnp.where(rank < take_eq, 0.0, NEG)
            b_ref[:, pl.ds(off, tk)] = jnp.where(blk > thr, 0.0, jnp.where(blk == thr, keep_eq, NEG))
            return seen + jnp.sum(eq, axis=1, keepdims=True)

        lax.fori_loop(0, n_kt, body, jnp.zeros((tq, 1), f32))

    pl.when(tie_excess > 0.0)(bias_ties)
    pl.when(tie_excess <= 0.0)(bias_plain)

    rep = ATTN_HEADS // KV_HEADS
    for g in range(KV_HEADS):
        qg = q_ref[g * rep:(g + 1) * rep].reshape(rep * tq, HEAD_DIM)

        def attend(kt, carry, g=g, qg=qg):
            m, l, acc = carry
            off = tile_off(kt)
            k_t = k_ref[0, pl.ds(off, tk), g * HEAD_DIM:(g + 1) * HEAD_DIM]
            v_t = v_ref[0, pl.ds(off, tk), g * HEAD_DIM:(g + 1) * HEAD_DIM]
            s = _dot_nt(qg, k_t).reshape(rep, tq, tk) + b_ref[:, pl.ds(off, tk)][None]
            s = s.reshape(rep * tq, tk)
            m_new = jnp.maximum(m, jnp.max(s, axis=1, keepdims=True))
            alpha = jnp.exp(m - m_new)
            p = jnp.exp(s - m_new)
            l = alpha * l + jnp.sum(p, axis=1, keepdims=True)
            acc = alpha * acc + _dot(p.astype(bf16), v_t)
            return m_new, l, acc

        init = (jnp.full((rep * tq, 1), NEG, f32), jnp.zeros((rep * tq, 1), f32), jnp.zeros((rep * tq, HEAD_DIM), f32))
        _, l, acc = lax.fori_loop(0, n_kt, attend, init)
        o = acc / l
        for r in range(rep):
            h = g * rep + r
            o_ref[:, h * HEAD_DIM:(h + 1) * HEAD_DIM] = o[r * tq:(r + 1) * tq].astype(o_ref.dtype)


def _attention(q_hm, qi_hm, w_s, k_b, v_b, ki_b, *, tq, tk, n_keys, q_base, q_stride, per_batch_keys):
    t = q_hm.shape[1]
    s_pad = k_b.shape[1]
    kidx = (lambda i: (i, 0, 0)) if per_batch_keys else (lambda i: (0, 0, 0))
    mode = pl.Buffered(2) if per_batch_keys else pl.Buffered(1)
    kern = functools.partial(_attn_kernel, tq=tq, tk=tk, n_keys=n_keys, q_base=q_base, q_stride=q_stride)
    return pl.pallas_call(
        kern,
        grid=(t // tq,),
        in_specs=[pl.BlockSpec((ATTN_HEADS, tq, HEAD_DIM), lambda i: (0, i, 0)),
                  pl.BlockSpec((IDX_HEADS, tq, IDX_DIM), lambda i: (0, i, 0)),
                  pl.BlockSpec((tq, LANES_V7X), lambda i: (i, 0)),
                  pl.BlockSpec((1, s_pad, KV_WIDTH), kidx, pipeline_mode=mode),
                  pl.BlockSpec((1, s_pad, KV_WIDTH), kidx, pipeline_mode=mode),
                  pl.BlockSpec((1, s_pad, IDX_DIM), kidx, pipeline_mode=mode)],
        out_specs=pl.BlockSpec((tq, ATTN_WIDTH), lambda i: (i, 0)),
        out_shape=jax.ShapeDtypeStruct((t, ATTN_WIDTH), bf16),
        scratch_shapes=[pltpu.VMEM((tq, s_pad), i32), pltpu.VMEM((tq, s_pad), f32)],
        compiler_params=_params(("arbitrary",), 56 * MIB),
        name="dsa_attn",
    )(q_hm, qi_hm, w_s, k_b, v_b, ki_b)


def _ssd_kernel(z_ref, xbc_ref, dt_ref, cst_ref, sst_ref, cw_ref, cb_ref, dtb_ref, alog_ref, dsk_ref, ng_ref, e_ref,
                y_ref, ncv_ref, nss_ref, ext_ref, st_ref, *, q_len, n_chunk):
    cs = q_len * n_chunk
    step = pl.program_id(1)
    hp = SSD_WIDTH
    gw = hp // SSD_GROUPS

    @pl.when(step == 0)
    def _():
        ext_ref[0:SUBLANES_V7X, :] = jnp.zeros((SUBLANES_V7X, CONV_DIM), f32)
        ext_ref[SUBLANES_V7X - (CONV_W - 1):SUBLANES_V7X, :] = cst_ref[0]
        st_ref[...] = sst_ref[0].T

    xbc = xbc_ref[...]
    ext_ref[SUBLANES_V7X:SUBLANES_V7X + cs, :] = xbc
    cw = cw_ref[...]
    conv = cb_ref[...] + ext_ref[SUBLANES_V7X - 3:SUBLANES_V7X - 3 + cs, :] * cw[0:1]
    conv = conv + ext_ref[SUBLANES_V7X - 2:SUBLANES_V7X - 2 + cs, :] * cw[1:2]
    conv = conv + ext_ref[SUBLANES_V7X - 1:SUBLANES_V7X - 1 + cs, :] * cw[2:3]
    conv = conv + xbc * cw[3:4]
    tail = xbc[cs - (CONV_W - 1):cs, :]
    ext_ref[SUBLANES_V7X - (CONV_W - 1):SUBLANES_V7X, :] = tail
    ncv_ref[0] = tail
    xc = conv * jax.nn.sigmoid(conv)
    xs = xc[:, :SSD_WIDTH]
    bm = xc[:, SSD_WIDTH:SSD_WIDTH + SSD_GROUPS * SSD_STATE]
    cm = xc[:, SSD_WIDTH + SSD_GROUPS * SSD_STATE:]

    pre = dt_ref[...] + dtb_ref[...]
    dt = jnp.maximum(pre, 0.0) + jnp.log1p(jnp.exp(-jnp.abs(pre)))
    a = dt * (-jnp.exp(alog_ref[...]))

    e = e_ref[...]

    def expand(v):
        hi, mid, lo = _split3(v)
        return _dot(hi, e) + _dot(mid, e) + _dot(lo, e)

    rows = lax.broadcasted_iota(i32, (q_len, LANES_V7X), 0)
    tri = lax.broadcasted_iota(i32, (q_len, q_len), 0) >= lax.broadcasted_iota(i32, (q_len, q_len), 1)
    y_parts = []
    for c in range(n_chunk):
        sl = slice(c * q_len, (c + 1) * q_len)
        ac = a[sl]
        shift = 1
        while shift < q_len:
            ac = ac + jnp.where(rows >= shift, pltpu.roll(ac, shift, 0), 0.0)
            shift *= 2
        alast = ac[q_len - 1:q_len]
        pad = max(LANES_V7X, q_len) - q_len
        ac_t = (jnp.concatenate([ac, jnp.zeros((pad, LANES_V7X), f32)], axis=0) if pad else ac).T
        xs_c = xs[sl]
        xdt = xs_c * expand(dt[sl])
        e_off = expand(jnp.exp(ac))
        xdec = (xdt * expand(jnp.exp(alast - ac))).astype(bf16)
        cdec = expand(jnp.broadcast_to(jnp.exp(alast), (SUBLANES_V7X, LANES_V7X)))[0:1]
        xdt_b = xdt.astype(bf16)
        y_c = []
        for g in range(SSD_GROUPS):
            b_g = bm[sl, g * SSD_STATE:(g + 1) * SSD_STATE].astype(bf16)
            c_g = cm[sl, g * SSD_STATE:(g + 1) * SSD_STATE].astype(bf16)
            cb = _dot_nt(c_g, b_g)
            st_g = st_ref[:, g * gw:(g + 1) * gw]
            y_off = _dot(c_g, st_g.astype(bf16)) * e_off[:, g * gw:(g + 1) * gw]
            new_s = lax.dot_general(b_g, xdec[:, g * gw:(g + 1) * gw], (((0,), (0,)), ((), ())),
                                    preferred_element_type=f32)
            st_ref[:, g * gw:(g + 1) * gw] = st_g * cdec[:, g * gw:(g + 1) * gw] + new_s
            diag = []
            for r in range(SSD_HEADS // SSD_GROUPS):
                h = g * (SSD_HEADS // SSD_GROUPS) + r
                seg = ac[:, h:h + 1] - ac_t[h:h + 1, :q_len]
                lmat = jnp.exp(jnp.where(tri, seg, -jnp.inf))
                diag.append(_dot((cb * lmat).astype(bf16), xdt_b[:, h * SSD_HEADDIM:(h + 1) * SSD_HEADDIM]))
            y_c.append(jnp.concatenate(diag, axis=1) + y_off)
        y_parts.append(jnp.concatenate(y_c, axis=1) + dsk_ref[...] * xs_c)
    y = jnp.concatenate(y_parts, axis=0) if n_chunk > 1 else y_parts[0]

    z = z_ref[...]
    y = y * (z * jax.nn.sigmoid(z))
    outs = []
    for g in range(SSD_GROUPS):
        yg = y[:, g * gw:(g + 1) * gw]
        outs.append(yg * lax.rsqrt(jnp.mean(yg * yg, axis=-1, keepdims=True) + EPS))
    y_ref[...] = (jnp.concatenate(outs, axis=1) * ng_ref[...]).astype(y_ref.dtype)

    @pl.when(step == pl.num_programs(1) - 1)
    def _():
        nss_ref[0] = st_ref[...].T


def _ssd(u, row0, batch, t_len, conv_state, ssm_state, conv_w, conv_b, dt_bias, a_log, d_skip, norm_g, *, q_len, n_chunk):
    cs = q_len * n_chunk
    steps = t_len // cs
    base = row0 // cs
    hp = SSD_WIDTH

    def ublk(width, off):
        return pl.BlockSpec((cs, width), lambda b, s, o=off // width: (base + b * steps + s, o))

    pad16 = lambda v: jnp.pad(v.astype(f32), (0, LANES_V7X - SSD_HEADS)).reshape(1, LANES_V7X)
    expand = jnp.repeat(jnp.eye(LANES_V7X, SSD_HEADS, dtype=bf16), SSD_HEADDIM, axis=1)
    const = lambda shape: pl.BlockSpec(shape, lambda b, s: tuple(0 for _ in shape))
    kern = functools.partial(_ssd_kernel, q_len=q_len, n_chunk=n_chunk)
    return pl.pallas_call(
        kern,
        grid=(batch, steps),
        in_specs=[ublk(SSD_WIDTH, U_Z), ublk(CONV_DIM, U_XBC), ublk(LANES_V7X, U_DT),
                  pl.BlockSpec((1, CONV_W - 1, CONV_DIM), lambda b, s: (b, 0, 0)),
                  pl.BlockSpec((1, hp, SSD_STATE), lambda b, s: (b, 0, 0)),
                  const((CONV_W, CONV_DIM)), const((1, CONV_DIM)), const((1, LANES_V7X)), const((1, LANES_V7X)),
                  const((1, hp)), const((1, hp)), const((LANES_V7X, hp))],
        out_specs=[pl.BlockSpec((cs, hp), lambda b, s: (b * steps + s, 0)),
                   pl.BlockSpec((1, CONV_W - 1, CONV_DIM), lambda b, s: (b, 0, 0)),
                   pl.BlockSpec((1, hp, SSD_STATE), lambda b, s: (b, 0, 0))],
        out_shape=[jax.ShapeDtypeStruct((batch * t_len, hp), bf16),
                   jax.ShapeDtypeStruct((batch, CONV_W - 1, CONV_DIM), f32),
                   jax.ShapeDtypeStruct((batch, hp, SSD_STATE), f32)],
        scratch_shapes=[pltpu.VMEM((SUBLANES_V7X + cs, CONV_DIM), f32), pltpu.VMEM((SSD_STATE, hp), f32)],
        compiler_params=_params(("arbitrary", "arbitrary"), 40 * MIB),
        name="ssd_mixer",
    )(u, u, u, conv_state, ssm_state.reshape(batch, hp, SSD_STATE), conv_w, conv_b.reshape(1, CONV_DIM),
      pad16(dt_bias), pad16(a_log), jnp.repeat(d_skip.astype(f32), SSD_HEADDIM).reshape(1, hp),
      norm_g.reshape(1, hp), expand)


def _outproj_kernel(oa_ref, os_ref, x_ref, ga_ref, scm_ref, shm_ref, g_ref, wo_ref, wr_ref, br_ref,
                    x1_ref, h2_ref, idx_ref, gate_ref, rank_ref, cnt_ref, carry_ref):
    step = pl.program_id(0)

    @pl.when(step == 0)
    def _():
        carry_ref[...] = jnp.zeros(carry_ref.shape, f32)

    half = wo_ref.shape[0] // 2
    mix = _dot(oa_ref[...], wo_ref[0:half, :]) + _dot(os_ref[...], wo_ref[half:, :])
    x = x_ref[...]
    gb = x.shape[0]
    x1 = x + ga_ref[...] * mix.reshape(x.shape)
    x1_ref[...] = x1.reshape(x1_ref.shape)
    y = x1 * lax.rsqrt(jnp.mean(x1 * x1, axis=-1, keepdims=True) + EPS) * g_ref[...]
    h2 = (y * (1.0 + scm_ref[...]) + shm_ref[...]).reshape(h2_ref.shape)
    h2_ref[...] = h2

    h_hi, h_mid, _ = _split3(h2)
    wr = wr_ref[...]
    w_hi, w_mid, _ = _split3(wr)
    logits = (_dot(h_hi, w_hi) + _dot(h_hi, w_mid) + _dot(h_mid, w_hi)) + br_ref[...]
    tm = logits.shape[0]
    lane = lax.broadcasted_iota(i32, (tm, LANES_V7X), 1)
    work = logits
    vals, idxs = [], []
    for _ in range(TOP_K):
        m = jnp.max(work, axis=1, keepdims=True)
        pick = jnp.min(jnp.where(work == m, lane, LANES_V7X), axis=1, keepdims=True)
        vals.append(m)
        idxs.append(pick)
        work = jnp.where(lane == pick, -jnp.inf, work)
    exps = [jnp.exp(v - vals[0]) for v in vals]
    denom = exps[0] + exps[1] + exps[2] + exps[3]
    idx_out = jnp.zeros((tm, LANES_V7X), i32)
    gate_out = jnp.zeros((tm, LANES_V7X), f32)
    onehot = jnp.zeros((tm, LANES_V7X), f32)
    for k in range(TOP_K):
        idx_out = jnp.where(lane == k, idxs[k], idx_out)
        gate_out = jnp.where(lane == k, exps[k] / denom, gate_out)
        onehot = onehot + jnp.where(lane == idxs[k], 1.0, 0.0)
    idx_ref[...] = idx_out
    gate_ref[...] = gate_out

    earlier = lax.broadcasted_iota(i32, (tm, tm), 1) < lax.broadcasted_iota(i32, (tm, tm), 0)
    before = _dot(jnp.where(earlier, 1.0, 0.0).astype(bf16), onehot.astype(bf16)) + carry_ref[0:1, :]
    rank_out = jnp.zeros((tm, LANES_V7X), f32)
    for k in range(TOP_K):
        rk = jnp.sum(jnp.where(lane == idxs[k], before, 0.0), axis=1, keepdims=True)
        rank_out = jnp.where(lane == k, rk, rank_out)
    rank_ref[...] = rank_out.astype(i32)
    total = carry_ref[0:1, :] + jnp.sum(onehot, axis=0, keepdims=True)
    carry_ref[...] = jnp.broadcast_to(total, carry_ref.shape)
    cnt_ref[...] = jnp.broadcast_to(total, cnt_ref.shape).astype(i32)


def _outproj(o_attn, o_ssd, x_g, ga_g, scm_g, shm_g, gain, w_out_b, w_router_p, b_router_p):
    ng, _, d = x_g.shape
    t = ng * GROUP
    tm = OUT_TM
    gb = tm // GROUP
    grp = lambda w: pl.BlockSpec((gb, w, d), lambda i: (i, 0, 0))
    row = lambda w: pl.BlockSpec((tm, w), lambda i: (i, 0))
    const = lambda shape, **kw: pl.BlockSpec(shape, lambda i: tuple(0 for _ in shape), **kw)
    return pl.pallas_call(
        _outproj_kernel,
        grid=(t // tm,),
        in_specs=[row(ATTN_WIDTH), row(SSD_WIDTH), grp(GROUP), grp(1), grp(1), grp(1), const((1, 1, d)),
                  const((d, d), pipeline_mode=pl.Buffered(1)), const((d, LANES_V7X)), const((1, LANES_V7X))],
        out_specs=[row(d), row(d), row(LANES_V7X), row(LANES_V7X), row(LANES_V7X),
                   pl.BlockSpec((SUBLANES_V7X, LANES_V7X), lambda i: (0, 0))],
        out_shape=[jax.ShapeDtypeStruct((t, d), f32), jax.ShapeDtypeStruct((t, d), f32),
                   jax.ShapeDtypeStruct((t, LANES_V7X), i32), jax.ShapeDtypeStruct((t, LANES_V7X), f32),
                   jax.ShapeDtypeStruct((t, LANES_V7X), i32), jax.ShapeDtypeStruct((SUBLANES_V7X, LANES_V7X), i32)],
        scratch_shapes=[pltpu.VMEM((SUBLANES_V7X, LANES_V7X), f32)],
        compiler_params=_params(("arbitrary",), 48 * MIB),
        name="out_proj_router",
    )(o_attn, o_ssd, x_g, ga_g, scm_g, shm_g, gain.reshape(1, 1, d), w_out_b, w_router_p, b_router_p)


def _dispatch_kernel(pad_lo_ref, pad_hi_ref, tail_ref, dest_hbm, h_ref, xs_hbm, dest_smem, zero_ref, sem, isem):
    step = pl.program_id(0)
    tt = h_ref.shape[0]
    idx_cp = pltpu.make_async_copy(dest_hbm.at[step], dest_smem, isem)
    idx_cp.start()

    def row_copy(src, dst_row):
        return pltpu.make_async_copy(src, xs_hbm.at[pl.ds(dst_row, 1), :], sem)

    def tail_copy(j):
        r = pl.multiple_of(j * MOE_SUB, MOE_SUB)
        return pltpu.make_async_copy(zero_ref, xs_hbm.at[pl.ds(r, MOE_SUB), :], sem)

    @pl.when(step == 0)
    def _():
        zero_ref[...] = jnp.zeros(zero_ref.shape, f32)
        n_tail_sub = xs_hbm.shape[0] // MOE_SUB

        def tail_start(j, c):
            tail_copy(j).start()
            return c

        def tail_wait(j, c):
            tail_copy(j).wait()
            return c

        lax.fori_loop(tail_ref[0], n_tail_sub, tail_start, 0)
        lax.fori_loop(tail_ref[0], n_tail_sub, tail_wait, 0)
        for e in range(N_EXPERTS):
            lo, hi = pad_lo_ref[e], pad_hi_ref[e]

            def fill(r, c):
                row_copy(zero_ref.at[pl.ds(0, 1), :], r).start()
                return c

            lax.fori_loop(lo, hi, fill, 0)

            def drain(r, c):
                row_copy(zero_ref.at[pl.ds(0, 1), :], r).wait()
                return c

            lax.fori_loop(lo, hi, drain, 0)

    idx_cp.wait()

    def issue(r, c):
        for k in range(TOP_K):
            row_copy(h_ref.at[pl.ds(r, 1), :], dest_smem[r * TOP_K + k]).start()
        return c

    lax.fori_loop(0, tt, issue, 0)

    def drain_rows(r, c):
        for k in range(TOP_K):
            row_copy(h_ref.at[pl.ds(r, 1), :], 0).wait()
        return c

    lax.fori_loop(0, tt, drain_rows, 0)


def _dispatch(h2, dest_tiles, pad_lo, pad_hi, tail_sub, n_rows):
    t, d = h2.shape
    tt = TOK_TILE
    return pl.pallas_call(
        _dispatch_kernel,
        grid_spec=pltpu.PrefetchScalarGridSpec(
            num_scalar_prefetch=3,
            grid=(t // tt,),
            in_specs=[pl.BlockSpec(memory_space=pl.ANY),
                      pl.BlockSpec((tt, d), lambda i, lo, hi, tl: (i, 0))],
            out_specs=pl.BlockSpec(memory_space=pl.ANY),
            scratch_shapes=[pltpu.SMEM((tt * TOP_K,), i32), pltpu.VMEM((MOE_SUB, d), f32),
                            pltpu.SemaphoreType.DMA, pltpu.SemaphoreType.DMA]),
        out_shape=jax.ShapeDtypeStruct((n_rows, d), f32),
        compiler_params=_params(("arbitrary",), 24 * MIB),
        name="moe_dispatch",
    )(pad_lo, pad_hi, tail_sub, dest_tiles, h2)


def _moe_kernel(st_e_ref, st_row_ref, st_n_ref, tail_ref, xs_hbm, w1g_ref, w1u_ref, b1g_ref, b1u_ref, w2_ref, b2_ref,
                ys_hbm, x_ref, acc_ref, wg_ref, wu_ref, wd_ref, sem_in, sem_out):
    s = pl.program_id(0)
    f = pl.program_id(1)
    nf = pl.num_programs(1)
    n_sub = st_n_ref[s]
    row0 = st_row_ref[s]

    @pl.when((s == 0) & (f == 0))
    def _():
        acc_ref[0] = jnp.zeros(acc_ref.shape[1:], f32)

        def tail_copy(j):
            r = pl.multiple_of(j * MOE_SUB, MOE_SUB)
            return pltpu.make_async_copy(acc_ref.at[0], ys_hbm.at[pl.ds(r, MOE_SUB), :], sem_out)

        def start(j, c):
            tail_copy(j).start()
            return c

        def wait(j, c):
            tail_copy(j).wait()
            return c

        lax.fori_loop(tail_ref[0], ys_hbm.shape[0] // MOE_SUB, start, 0)
        lax.fori_loop(tail_ref[0], ys_hbm.shape[0] // MOE_SUB, wait, 0)

    def in_copy(j):
        r = pl.multiple_of(row0 + j * MOE_SUB, MOE_SUB)
        return pltpu.make_async_copy(xs_hbm.at[pl.ds(r, MOE_SUB), :], x_ref.at[j], sem_in)

    def out_copy(j):
        r = pl.multiple_of(row0 + j * MOE_SUB, MOE_SUB)
        return pltpu.make_async_copy(acc_ref.at[j], ys_hbm.at[pl.ds(r, MOE_SUB), :], sem_out)

    @pl.when(n_sub > 0)
    def _():
        @pl.when(f == 0)
        def _():
            def start(j, c):
                in_copy(j).start()
                return c

            lax.fori_loop(0, n_sub, start, 0)

        wg_ref[...] = w1g_ref[0].astype(bf16)
        wu_ref[...] = w1u_ref[0].astype(bf16)
        wd_ref[...] = w2_ref[0].astype(bf16)

        @pl.when(f == 0)
        def _():
            def wait(j, c):
                in_copy(j).wait()
                return c

            lax.fori_loop(0, n_sub, wait, 0)

        def sub_tile(j, c):
            x = x_ref[j].astype(bf16)
            gate = _dot(x, wg_ref[...]) + b1g_ref[0]
            up = _dot(x, wu_ref[...]) + b1u_ref[0]
            gate = jnp.minimum(gate, SWIGLU_LIMIT)
            up = jnp.clip(up, -SWIGLU_LIMIT, SWIGLU_LIMIT)
            act = gate * jax.nn.sigmoid(SWIGLU_ALPHA * gate) * (up + 1.0)
            part = _dot(act.astype(bf16), wd_ref[...])

            @pl.when(f == 0)
            def _():
                acc_ref[j] = part + b2_ref[0]

            @pl.when(f > 0)
            def _():
                acc_ref[j] = acc_ref[j] + part

            return c

        lax.fori_loop(0, n_sub, sub_tile, 0)

        @pl.when(f == nf - 1)
        def _():
            def start(j, c):
                out_copy(j).start()
                return c

            lax.fori_loop(0, n_sub, start, 0)

            def wait(j, c):
                out_copy(j).wait()
                return c

            lax.fori_loop(0, n_sub, wait, 0)


def _moe(xs, st_e, st_row, st_n, tail_sub, w1, b1, w2, b2):
    n_rows, d = xs.shape
    n_super = st_e.shape[0]
    nf = D_FF // MOE_FFC

    def fchunk(f, n):
        return jnp.where(n > 0, f, nf - 1)

    return pl.pallas_call(
        _moe_kernel,
        grid_spec=pltpu.PrefetchScalarGridSpec(
            num_scalar_prefetch=4,
            grid=(n_super, nf),
            in_specs=[pl.BlockSpec(memory_space=pl.ANY),
                      pl.BlockSpec((1, d, MOE_FFC), lambda s, f, e, r, n, tl: (e[s], 0, fchunk(f, n[s]))),
                      pl.BlockSpec((1, d, MOE_FFC), lambda s, f, e, r, n, tl: (e[s], 0, nf + fchunk(f, n[s]))),
                      pl.BlockSpec((1, 1, MOE_FFC), lambda s, f, e, r, n, tl: (e[s], 0, fchunk(f, n[s]))),
                      pl.BlockSpec((1, 1, MOE_FFC), lambda s, f, e, r, n, tl: (e[s], 0, nf + fchunk(f, n[s]))),
                      pl.BlockSpec((1, MOE_FFC, d), lambda s, f, e, r, n, tl: (e[s], fchunk(f, n[s]), 0)),
                      pl.BlockSpec((1, 1, d), lambda s, f, e, r, n, tl: (e[s], 0, 0))],
            out_specs=pl.BlockSpec(memory_space=pl.ANY),
            scratch_shapes=[pltpu.VMEM((MOE_SUPER, MOE_SUB, d), f32), pltpu.VMEM((MOE_SUPER, MOE_SUB, d), f32),
                            pltpu.VMEM((d, MOE_FFC), bf16), pltpu.VMEM((d, MOE_FFC), bf16), pltpu.VMEM((MOE_FFC, d), bf16),
                            pltpu.SemaphoreType.DMA, pltpu.SemaphoreType.DMA]),
        out_shape=jax.ShapeDtypeStruct((n_rows, d), f32),
        compiler_params=_params(("arbitrary", "arbitrary"), 58 * MIB),
        name="moe_experts",
    )(st_e, st_row, st_n, tail_sub, xs, w1, w1, b1.reshape(N_EXPERTS, 1, 2 * D_FF), b1.reshape(N_EXPERTS, 1, 2 * D_FF), w2,
      b2.reshape(N_EXPERTS, 1, d))


def _combine_kernel(dest_hbm, ys_hbm, x1_ref, gate_ref, gm_ref, o_ref, dest_smem, buf_ref, sem, isem):
    step = pl.program_id(0)
    tt = x1_ref.shape[0]
    idx_cp = pltpu.make_async_copy(dest_hbm.at[step], dest_smem, isem)
    idx_cp.start()
    idx_cp.wait()

    def row_copy(r, k, src_row):
        return pltpu.make_async_copy(ys_hbm.at[pl.ds(src_row, 1), :], buf_ref.at[k, pl.ds(r, 1), :], sem)

    def issue(r, c):
        for k in range(TOP_K):
            row_copy(r, k, dest_smem[r * TOP_K + k]).start()
        return c

    lax.fori_loop(0, tt, issue, 0)

    def drain(r, c):
        for k in range(TOP_K):
            row_copy(r, k, 0).wait()
        return c

    lax.fori_loop(0, tt, drain, 0)

    gate = gate_ref[...]
    moe = buf_ref[0] * gate[:, 0:1]
    for k in range(1, TOP_K):
        moe = moe + buf_ref[k] * gate[:, k:k + 1]
    gb = gm_ref.shape[0]
    o_ref[...] = (x1_ref[...].reshape(gb, GROUP, -1) + gm_ref[...] * moe.reshape(gb, GROUP, -1)).reshape(o_ref.shape)


def _combine(dest_tiles, ys, x1, gates, gm_g):
    t, d = x1.shape
    tt = TOK_TILE
    gb = tt // GROUP
    return pl.pallas_call(
        _combine_kernel,
        grid=(t // tt,),
        in_specs=[pl.BlockSpec(memory_space=pl.ANY), pl.BlockSpec(memory_space=pl.ANY),
                  pl.BlockSpec((tt, d), lambda i: (i, 0)), pl.BlockSpec((tt, LANES_V7X), lambda i: (i, 0)),
                  pl.BlockSpec((gb, 1, d), lambda i: (i, 0, 0))],
        out_specs=pl.BlockSpec((tt, d), lambda i: (i, 0)),
        out_shape=jax.ShapeDtypeStruct((t, d), f32),
        scratch_shapes=[pltpu.SMEM((tt * TOP_K,), i32), pltpu.VMEM((TOP_K, tt, d), f32),
                        pltpu.SemaphoreType.DMA, pltpu.SemaphoreType.DMA],
        compiler_params=_params(("arbitrary",), 32 * MIB),
        name="moe_combine",
    )(dest_tiles, ys, x1, gates, gm_g)


def _pack_w_in(w_in):
    sizes = (ATTN_WIDTH, KV_WIDTH, KV_WIDTH, IDX_HEADS * IDX_DIM, IDX_DIM, IDX_HEADS, SSD_WIDTH, CONV_DIM, SSD_HEADS)
    q, k, v, qi, ki, wi, z, xbc, dt = jnp.split(w_in, np.cumsum(sizes)[:-1].tolist(), axis=1)
    padl = lambda a: jnp.pad(a, ((0, 0), (0, LANES_V7X - a.shape[1])))
    tail = jnp.zeros((w_in.shape[0], U_COLS - U_DT - LANES_V7X), w_in.dtype)
    return jnp.concatenate([q, qi, z, xbc, k, v, padl(ki), padl(wi), padl(dt), tail], axis=1).astype(bf16)


def _rope_tables(pos):
    def table(dim):
        half = dim // 2
        inv_freq = ROPE_THETA ** (-jnp.arange(half, dtype=f32) / half)
        ang = pos.astype(f32)[:, None] * inv_freq[None, :]
        cos, sin = jnp.cos(ang), jnp.sin(ang)
        reps = LANES_V7X // dim
        return jnp.tile(jnp.concatenate([cos, cos], axis=1), (1, reps)), jnp.tile(jnp.concatenate([-sin, sin], axis=1), (1, reps))

    cq, sq = table(HEAD_DIM)
    ci, si = table(IDX_DIM)
    return cq, sq, ci, si


def _moe_schedule(counts, n_tok):
    n_sub = (counts + MOE_SUB - 1) // MOE_SUB
    sub_end = jnp.cumsum(n_sub)
    pstart = (sub_end - n_sub) * MOE_SUB
    n_st = (n_sub + MOE_SUPER - 1) // MOE_SUPER
    per = (n_sub + jnp.maximum(n_st, 1) - 1) // jnp.maximum(n_st, 1)
    st_end = jnp.cumsum(n_st)
    total_sub_max = (n_tok * TOP_K) // MOE_SUB + N_EXPERTS
    n_super = total_sub_max // MOE_SUPER + N_EXPERTS
    s = jnp.arange(n_super, dtype=i32)
    owner = lambda q: jnp.minimum(jnp.sum((st_end[None, :] <= q[:, None]).astype(i32), axis=1), N_EXPERTS - 1)
    e = owner(s)
    active = s < st_end[-1]
    last_e = owner(st_end[-1:] - 1)[0]
    e = jnp.where(active, e, last_e)
    local = s - (st_end[e] - n_st[e])
    sub0 = local * per[e]
    n = jnp.where(active, jnp.clip(n_sub[e] - sub0, 0, per[e]), 0).astype(i32)
    row = jnp.where(active, pstart[e] + sub0 * MOE_SUB, 0).astype(i32)
    n_rows = total_sub_max * MOE_SUB
    pad_lo = (pstart + counts).astype(i32)
    pad_hi = (pstart + n_sub * MOE_SUB).astype(i32)
    tail_sub = sub_end[-1:].astype(i32)
    return pstart.astype(i32), e, row, n, pad_lo, pad_hi, tail_sub, n_rows


def kernel(x_prompt, x_sample, c_prompt, c_sample, cache_k, cache_v, cache_idx_k, state_conv, state_ssm, w_ada, b_ada,
           mix_norm_g, ffn_norm_g, w_in, q_norm_g, k_norm_g, conv_w, conv_b, dt_bias, a_log, d_skip, ssd_norm_g, w_out,
           w_router, b_router, w_mlp1, b_mlp1, w_mlp2, b_mlp2):
    depth = w_ada.shape[0]
    assert depth == 1
    b_p, t_p, d = x_prompt.shape
    b_s, t_s, _ = x_sample.shape
    assert b_p == 1 and t_s == GROUP and d == D_MODEL
    past = cache_k.shape[2]
    n_p = b_p * t_p
    n_s = b_s * t_s
    n_tok = n_p + n_s
    l = 0

    c_all = jnp.concatenate([c_prompt, c_sample, jnp.zeros((SUBLANES_V7X - (b_p + b_s) % SUBLANES_V7X, d), f32)], axis=0)
    mod = _ada(c_all, w_ada[l], b_ada[l])
    grp_row = jnp.concatenate([jnp.zeros((n_p // GROUP,), i32), b_p + jnp.arange(b_s, dtype=i32)])
    sh_a, sc_a, g_a, sh_m, sc_m, g_m = [m[grp_row][:, None, :] for m in jnp.split(mod, 6, axis=1)]

    x_g = jnp.concatenate([x_prompt.reshape(n_p // GROUP, GROUP, d), x_sample], axis=0)

    u = _inproj(x_g, sc_a, sh_a, mix_norm_g[l], _pack_w_in(w_in[l]))
    pos = jnp.concatenate([jnp.arange(t_p, dtype=i32), jnp.tile(past + jnp.arange(t_s, dtype=i32), b_s)])
    q_hm, qi_hm, k_f, k_b, v_f, v_b, ki_f, ki_b, w_s = _post(u, *_rope_tables(pos), q_norm_g[l], k_norm_g[l])

    o_attn_p = _attention(q_hm[:, :n_p], qi_hm[:, :n_p], w_s[:n_p], k_b[None, :n_p], v_b[None, :n_p], ki_b[None, :n_p],
                          tq=128, tk=512, n_keys=t_p, q_base=0, q_stride=128, per_batch_keys=False)
    tk_s = 384
    s_all = past + t_s
    s_pad = -(-s_all // tk_s) * tk_s

    def with_cache(cache, new):
        new = new[n_p:].reshape(b_s, t_s, -1)
        full = jnp.concatenate([cache.reshape(b_s, past, -1).astype(bf16), new], axis=1)
        return jnp.pad(full, ((0, 0), (0, s_pad - s_all), (0, 0)))

    o_attn_s = _attention(q_hm[:, n_p:], qi_hm[:, n_p:], w_s[n_p:], with_cache(cache_k[l], k_b), with_cache(cache_v[l], v_b),
                          with_cache(cache_idx_k[l], ki_b), tq=t_s, tk=tk_s, n_keys=s_all, q_base=past, q_stride=0,
                          per_batch_keys=True)
    o_attn = jnp.concatenate([o_attn_p, o_attn_s], axis=0)

    ssd_args = (conv_w[l], conv_b[l], dt_bias[l], a_log[l], d_skip[l], ssd_norm_g[l])
    o_ssd_p, conv_p, ssm_p = _ssd(u, 0, b_p, t_p, jnp.zeros((b_p, CONV_W - 1, CONV_DIM), f32),
                                  jnp.zeros((b_p, SSD_HEADS, SSD_HEADDIM, SSD_STATE), f32), *ssd_args, q_len=CHUNK, n_chunk=2)
    o_ssd_s, conv_s, ssm_s = _ssd(u, n_p, b_s, t_s, state_conv[l], state_ssm[l], *ssd_args, q_len=t_s, n_chunk=1)
    o_ssd = jnp.concatenate([o_ssd_p, o_ssd_s], axis=0)

    w_router_p = jnp.pad(w_router[l], ((0, 0), (0, LANES_V7X - N_EXPERTS)))
    b_router_p = jnp.concatenate([b_router[l], jnp.full((LANES_V7X - N_EXPERTS,), -jnp.inf, f32)]).reshape(1, LANES_V7X)
    x1, h2, top_idx, gates, rank, counts = _outproj(o_attn, o_ssd, x_g, g_a, sc_m, sh_m, ffn_norm_g[l], w_out[l].astype(bf16),
                                                    w_router_p, b_router_p)

    pstart, st_e, st_row, st_n, pad_lo, pad_hi, tail_sub, n_rows = _moe_schedule(counts[0, :N_EXPERTS], n_tok)
    dest = (pstart[top_idx[:, :TOP_K]] + rank[:, :TOP_K]).reshape(n_tok // TOK_TILE, TOK_TILE * TOP_K)
    xs = _dispatch(h2, dest, pad_lo, pad_hi, tail_sub, n_rows)
    ys = _moe(xs, st_e, st_row, st_n, tail_sub, w_mlp1[l], b_mlp1[l], w_mlp2[l], b_mlp2[l])
    y = _combine(dest, ys, x1, gates, g_m)

    lead = lambda a, shape: a.reshape((depth,) + shape)
    return (y[:n_p].reshape(b_p, t_p, d), y[n_p:].reshape(b_s, t_s, d),
            lead(k_f[:n_p], (b_p, t_p, KV_HEADS, HEAD_DIM)), lead(v_f[:n_p], (b_p, t_p, KV_HEADS, HEAD_DIM)),
            lead(ki_f[:n_p], (b_p, t_p, IDX_DIM)), lead(conv_p, (b_p, CONV_W - 1, CONV_DIM)),
            lead(ssm_p, (b_p, SSD_HEADS, SSD_HEADDIM, SSD_STATE)),
            lead(k_f[n_p:], (b_s, t_s, KV_HEADS, HEAD_DIM)), lead(v_f[n_p:], (b_s, t_s, KV_HEADS, HEAD_DIM)),
            lead(ki_f[n_p:], (b_s, t_s, IDX_DIM)), lead(conv_s, (b_s, CONV_W - 1, CONV_DIM)),
            lead(ssm_s, (b_s, SSD_HEADS, SSD_HEADDIM, SSD_STATE)))
```
